```python
import jax, jax.numpy as jnp
from jax import lax
import numpy as np

D_MODEL = 1024
BATCH = 4
SEQ = 4096
DEPTH = 1

N_META = 16
NORM_EPS = 1e-6
M_HEADS = 4
M_V = 2 * D_MODEL
M_V_HEAD = M_V // M_HEADS
M_QK_HEAD = M_V_HEAD // 2
M_QK = M_HEADS * M_QK_HEAD
CHUNK = 64
GATE_SOFTCAP = 15.0
S_WIDTH = 2 * D_MODEL
CONV_W = 3
IN_SPLITS = (M_QK, M_QK, M_V, M_V, M_V, M_HEADS, M_HEADS,
             S_WIDTH, S_WIDTH, S_WIDTH, S_WIDTH, D_MODEL, D_MODEL)
N_IN = 2 * M_QK + 3 * M_V + 2 * M_HEADS + 4 * S_WIDTH + 2 * D_MODEL

kernel_name = "meta_mlstm_shortconv_gated_hybrid"


def rmsnorm(x, w):
    xf = x.astype(jnp.float32)
    y = xf * lax.rsqrt(jnp.mean(xf * xf, axis=-1, keepdims=True) + NORM_EPS)
    return (y * w.astype(jnp.float32)).astype(x.dtype)


def softcap(x, cap):
    return cap * jnp.tanh(x / cap)


def causal_depthwise_conv(u, w):
    k, c = w.shape
    return lax.conv_general_dilated(
        u, w[:, None, :].astype(u.dtype), window_strides=(1,), padding=[(k - 1, 0)],
        dimension_numbers=("NWC", "WIO", "NWC"), feature_group_count=c)


def mlstm_chunkwise(q, k, v, li, lf):
    bsz, seq_len, n_heads, dk = q.shape
    dv = v.shape[-1]
    pad = CHUNK - N_META

    def padt(a, val):
        return jnp.pad(a, [(0, 0), (pad, 0)] + [(0, 0)] * (a.ndim - 2), constant_values=val)

    q = padt(q * (dk ** -0.5), 0.0)
    k = padt(k, 0.0)
    v = padt(v, 0.0)
    lf = padt(lf, 0.0)
    li = padt(li, -jnp.inf)
    n_chunks = (seq_len + pad) // CHUNK

    def to_chunks(a):
        a = a.reshape((bsz, n_chunks, CHUNK) + a.shape[2:])
        return jnp.moveaxis(a, (1, 3), (0, 2))

    xs = tuple(to_chunks(a) for a in (q, k, v, li, lf))
    causal = jnp.tril(jnp.ones((CHUNK, CHUNK), dtype=bool))

    def step(carry, inp):
        c_prev, n_prev, m_prev = carry
        qc, kc, vc, lic, lfc = inp
        b = jnp.cumsum(lfc, axis=-1)
        dlog = jnp.where(causal, b[..., :, None] - b[..., None, :] + lic[..., None, :], -jnp.inf)
        inter = b + m_prev[..., None]
        m_t = jnp.maximum(inter, jnp.max(dlog, axis=-1))
        w = jnp.exp(dlog - m_t[..., None]) * jnp.einsum("bhtd,bhsd->bhts", qc, kc)
        wi = jnp.exp(inter - m_t)
        num = (wi[..., None] * jnp.einsum("bhtd,bhde->bhte", qc, c_prev)
               + jnp.einsum("bhts,bhse->bhte", w, vc))
        den = wi * jnp.einsum("bhtd,bhd->bht", qc, n_prev) + jnp.sum(w, axis=-1)
        h = num / jnp.maximum(jnp.abs(den), jnp.exp(-m_t))[..., None]
        b_last = b[..., -1]
        wlog = b_last[..., None] - b + lic
        m_new = jnp.maximum(b_last + m_prev, jnp.max(wlog, axis=-1))
        decay = jnp.exp(b_last + m_prev - m_new)
        kw = jnp.exp(wlog - m_new[..., None])[..., None] * kc
        c_new = decay[..., None, None] * c_prev + jnp.einsum("bhsd,bhse->bhde", kw, vc)
        n_new = decay[..., None] * n_prev + jnp.sum(kw, axis=-2)
        return (c_new, n_new, m_new), h

    init = (jnp.zeros((bsz, n_heads, dk, dv), jnp.float32),
            jnp.zeros((bsz, n_heads, dk), jnp.float32),
            jnp.zeros((bsz, n_heads), jnp.float32))
    _, h = lax.scan(step, init, xs)
    h = jnp.moveaxis(h, (0, 2), (1, 3)).reshape(bsz, n_chunks * CHUNK, n_heads, dv)
    return h[:, pad:]


def hybrid_layer(x, norm_w, w_in, b_igate, b_fgate, mh_norm_w, conv_w,
                 w_proj_a, w_proj_b, w_out):
    bsz, seq_len, _ = x.shape
    f32 = jnp.float32
    xn = rmsnorm(x, norm_w)
    proj = xn @ w_in
    split_at = np.cumsum(IN_SPLITS)[:-1].tolist()
    (q, k, v, o_pre, z_a, i_pre, f_pre,
     s_h, s_b, s_c, z_s, g_a, g_b) = jnp.split(proj, split_at, axis=-1)

    qh = q.reshape(bsz, seq_len, M_HEADS, M_QK_HEAD).astype(f32)
    kh = k.reshape(bsz, seq_len, M_HEADS, M_QK_HEAD).astype(f32)
    vh = v.reshape(bsz, seq_len, M_HEADS, M_V_HEAD).astype(f32)
    li = softcap(i_pre.astype(f32) + b_igate.astype(f32), GATE_SOFTCAP)
    lf = jax.nn.log_sigmoid(softcap(f_pre.astype(f32) + b_fgate.astype(f32), GATE_SOFTCAP))
    h = mlstm_chunkwise(qh, kh, vh, li, lf)
    h = h * lax.rsqrt(jnp.mean(h * h, axis=-1, keepdims=True) + NORM_EPS)
    h = (h.reshape(bsz, seq_len, M_V) * mh_norm_w.astype(f32)).astype(x.dtype)
    a = jax.nn.silu(z_a) * (jax.nn.sigmoid(o_pre) * h)

    y_s = jax.nn.silu(z_s) * (s_b * causal_depthwise_conv(s_c * s_h, conv_w))

    merged = (jax.nn.sigmoid(g_a) * (a @ w_proj_a)
              + jax.nn.sigmoid(g_b) * (y_s @ w_proj_b))
    return x + merged @ w_out


def setup_inputs(seed: int = 0) -> dict:
    key = jax.random.key(seed)
    ks = jax.random.split(key, 13)
    nrm = jax.random.normal
    return {
        "x": nrm(ks[0], (BATCH, SEQ, D_MODEL), jnp.float32),
        "meta_tokens": nrm(ks[1], (N_META, D_MODEL), jnp.float32),
        "norm_w": 1.0 + 0.02 * nrm(ks[2], (DEPTH, D_MODEL), jnp.float32),
        "w_in": nrm(ks[3], (DEPTH, D_MODEL, N_IN), jnp.float32) * D_MODEL ** -0.5,
        "b_igate": 0.1 * nrm(ks[4], (DEPTH, M_HEADS), jnp.float32),
        "b_fgate": jnp.linspace(3.0, 6.0, M_HEADS, dtype=jnp.float32)[None, :]
                   + 0.1 * nrm(ks[5], (DEPTH, M_HEADS), jnp.float32),
        "mh_norm_w": 1.0 + 0.02 * nrm(ks[6], (DEPTH, M_V), jnp.float32),
        "conv_w": nrm(ks[7], (DEPTH, CONV_W, S_WIDTH), jnp.float32) * CONV_W ** -0.5,
        "w_proj_a": nrm(ks[8], (DEPTH, M_V, D_MODEL), jnp.float32) * M_V ** -0.5,
        "w_proj_b": nrm(ks[9], (DEPTH, S_WIDTH, D_MODEL), jnp.float32) * S_WIDTH ** -0.5,
        "w_out": nrm(ks[10], (DEPTH, D_MODEL, D_MODEL), jnp.float32) * D_MODEL ** -0.5,
        "final_norm_w": 1.0 + 0.02 * nrm(ks[11], (D_MODEL,), jnp.float32),
    }


def reference(x, meta_tokens, norm_w, w_in, b_igate, b_fgate, mh_norm_w, conv_w,
              w_proj_a, w_proj_b, w_out, final_norm_w):
    bsz = x.shape[0]
    meta = jnp.broadcast_to(meta_tokens[None].astype(x.dtype), (bsz, N_META, D_MODEL))
    h = jnp.concatenate([meta, x], axis=1)
    for layer in range(DEPTH):
        h = hybrid_layer(h, norm_w[layer], w_in[layer], b_igate[layer], b_fgate[layer],
                         mh_norm_w[layer], conv_w[layer], w_proj_a[layer],
                         w_proj_b[layer], w_out[layer])
    return rmsnorm(h, final_norm_w)[:, N_META:]
```

```python
import functools

import jax
import jax.numpy as jnp
from jax import lax
from jax.experimental import pallas as pl
from jax.experimental.pallas import tpu as pltpu

D_MODEL = 1024
N_META = 16
NORM_EPS = 1e-6
M_HEADS = 4
M_V = 2 * D_MODEL
M_V_HEAD = M_V // M_HEADS
M_QK_HEAD = M_V_HEAD // 2
M_QK = M_HEADS * M_QK_HEAD
GATE_SOFTCAP = 15.0
S_WIDTH = 2 * D_MODEL
CONV_W = 3

LANES = 128
GATE_PAD = LANES
N_GATES = 2 * M_HEADS
OFF_Q = 0
OFF_K = OFF_Q + M_QK
OFF_V = OFF_K + M_QK
OFF_O = OFF_V + M_V
OFF_ZA = OFF_O + M_V
OFF_SH = OFF_ZA + M_V
OFF_SB = OFF_SH + S_WIDTH
OFF_SC = OFF_SB + S_WIDTH
OFF_ZS = OFF_SC + S_WIDTH
OFF_GA = OFF_ZS + S_WIDTH
OFF_GB = OFF_GA + D_MODEL
N_MAIN = OFF_GB + D_MODEL
OFF_GATES_IN_W = OFF_ZA + M_V

VMEM_LIMIT_BYTES = 56 * 1024 * 1024

PROJ_TM = 1024
PROJ_TN = 2048
CHUNK = 256
MERGE_TM = 256
MERGE_CB = 512
CONV_HIST = 8

BF16 = jnp.bfloat16
F32 = jnp.float32


def _sigmoid(x):
    return 1.0 / (1.0 + jnp.exp(-x))


def _in_proj_kernel(x_ref, nw_ref, w_ref, wg_ref, proj_ref, gate_ref, xn_ref):
    @pl.when(pl.program_id(1) == 0)
    def _():
        x = x_ref[...]
        ms = jnp.mean(x * x, axis=-1, keepdims=True)
        xn = (x * lax.rsqrt(ms + NORM_EPS) * nw_ref[...]).astype(BF16)
        xn_ref[...] = xn
        gate_ref[...] = jnp.dot(xn, wg_ref[...], preferred_element_type=F32)

    proj_ref[...] = jnp.dot(xn_ref[...], w_ref[...],
                            preferred_element_type=F32).astype(BF16)


def _in_proj(x2d, norm_w, w_main, w_gate, tm):
    rows = x2d.shape[0]
    grid = (rows // tm, N_MAIN // PROJ_TN)
    return pl.pallas_call(
        _in_proj_kernel,
        out_shape=(jax.ShapeDtypeStruct((rows, N_MAIN), BF16),
                   jax.ShapeDtypeStruct((rows, GATE_PAD), F32)),
        grid=grid,
        in_specs=[
            pl.BlockSpec((tm, D_MODEL), lambda i, j: (i, 0)),
            pl.BlockSpec((1, D_MODEL), lambda i, j: (0, 0)),
            pl.BlockSpec((D_MODEL, PROJ_TN), lambda i, j: (0, j)),
            pl.BlockSpec((D_MODEL, GATE_PAD), lambda i, j: (0, 0)),
        ],
        out_specs=(pl.BlockSpec((tm, PROJ_TN), lambda i, j: (i, j)),
                   pl.BlockSpec((tm, GATE_PAD), lambda i, j: (i, 0))),
        scratch_shapes=[pltpu.VMEM((tm, D_MODEL), BF16)],
        compiler_params=pltpu.CompilerParams(
            dimension_semantics=("arbitrary", "arbitrary"),
            vmem_limit_bytes=VMEM_LIMIT_BYTES),
        name="in_proj",
    )(x2d, norm_w, w_main, w_gate)


def _split3(x):
    hi = x.astype(BF16)
    r1 = x - hi.astype(F32)
    mid = r1.astype(BF16)
    lo = (r1 - mid.astype(F32)).astype(BF16)
    return hi, mid, lo


def _log_gates(g, bias):
    z = GATE_SOFTCAP * jnp.tanh((g + bias) / GATE_SOFTCAP)
    log_f = jnp.minimum(z, 0.0) - jnp.log1p(jnp.exp(-jnp.abs(z)))
    lane = lax.broadcasted_iota(jnp.int32, g.shape, 1)
    return jnp.where(lane < M_HEADS, z, jnp.where(lane < N_GATES, log_f, 0.0))


def _cumsum_rows(a):
    t = a.shape[0]
    r = lax.broadcasted_iota(jnp.int32, (t, t), 0)
    c = lax.broadcasted_iota(jnp.int32, (t, t), 1)
    tri = jnp.where(r >= c, 1.0, 0.0).astype(BF16)
    hi, mid, lo = _split3(a)
    acc = jnp.dot(tri, lo, preferred_element_type=F32)
    acc = acc + jnp.dot(tri, mid, preferred_element_type=F32)
    return acc + jnp.dot(tri, hi, preferred_element_type=F32)


def _state_update(c_prev, n_prev, m_prev, k, v, li_col, b_col, b_last):
    wlog = b_last - b_col + li_col
    m_new = jnp.maximum(b_last + m_prev, jnp.max(wlog, axis=0, keepdims=True))
    decay = jnp.exp(b_last + m_prev - m_new)
    kw = jnp.exp(wlog - m_new) * k.astype(F32)
    c_new = decay * c_prev + lax.dot_general(
        kw.astype(BF16), v, (((0,), (0,)), ((), ())), preferred_element_type=F32)
    n_new = decay * n_prev + jnp.sum(kw, axis=0, keepdims=True)
    return c_new, n_new, m_new


def _mlstm_kernel(q_ref, k_ref, v_ref, o_ref, za_ref, g_ref,
                  km_ref, vm_ref, gm_ref, gbias_ref, mhw_ref,
                  a_ref, c_scr, n_scr, m_scr):
    t = CHUNK
    gbias = gbias_ref[...]

    @pl.when(pl.program_id(1) == 0)
    def _():
        la = _log_gates(gm_ref[...], gbias)
        cum = _cumsum_rows(la)
        for h in range(M_HEADS):
            li_col = la[:, h:h + 1]
            b_col = cum[:, M_HEADS + h:M_HEADS + h + 1]
            b_last = b_col[N_META - 1:N_META, :]
            k = km_ref[:, h * M_QK_HEAD:(h + 1) * M_QK_HEAD]
            v = vm_ref[:, h * M_V_HEAD:(h + 1) * M_V_HEAD]
            c_new, n_new, m_new = _state_update(
                jnp.zeros((M_QK_HEAD, M_V_HEAD), F32), jnp.zeros((1, M_QK_HEAD), F32),
                jnp.zeros((1, 1), F32), k, v, li_col, b_col, b_last)
            c_scr[h] = c_new
            n_scr[h] = n_new
            m_scr[h] = jnp.broadcast_to(m_new, (1, LANES))

    la = _log_gates(g_ref[...], gbias)
    cum = _cumsum_rows(la)
    la_t = la.T
    cum_t = cum.T
    row = lax.broadcasted_iota(jnp.int32, (t, t), 0)
    col = lax.broadcasted_iota(jnp.int32, (t, t), 1)
    causal = row >= col

    for h in range(M_HEADS):
        qs = slice(h * M_QK_HEAD, (h + 1) * M_QK_HEAD)
        vs = slice(h * M_V_HEAD, (h + 1) * M_V_HEAD)
        q = q_ref[:, qs] * jnp.asarray(M_QK_HEAD ** -0.5, BF16)
        k = k_ref[:, qs]
        v = v_ref[:, vs]
        c_prev = c_scr[h]
        n_prev = n_scr[h]
        m_prev = m_scr[h][:, 0:1]

        li_col = la[:, h:h + 1]
        b_col = cum[:, M_HEADS + h:M_HEADS + h + 1]
        li_row = la_t[h:h + 1, :]
        b_row = cum_t[M_HEADS + h:M_HEADS + h + 1, :]
        b_last = b_col[t - 1:t, :]

        s = lax.dot_general(q, k, (((1,), (1,)), ((), ())), preferred_element_type=F32)
        dlog = jnp.where(causal, b_col - b_row + li_row, -jnp.inf)
        inter = b_col + m_prev
        m_t = jnp.maximum(inter, jnp.max(dlog, axis=-1, keepdims=True))
        w = jnp.exp(dlog - m_t) * s
        wi = jnp.exp(inter - m_t)
        num = (wi * jnp.dot(q, c_prev.astype(BF16), preferred_element_type=F32)
               + jnp.dot(w.astype(BF16), v, preferred_element_type=F32))
        qn = jnp.sum(q.astype(F32) * n_prev, axis=-1, keepdims=True)
        den = wi * qn + jnp.sum(w, axis=-1, keepdims=True)
        hh = num / jnp.maximum(jnp.abs(den), jnp.exp(-m_t))

        c_new, n_new, m_new = _state_update(c_prev, n_prev, m_prev, k, v, li_col, b_col, b_last)
        c_scr[h] = c_new
        n_scr[h] = n_new
        m_scr[h] = jnp.broadcast_to(m_new, (1, LANES))

        hn = hh * lax.rsqrt(jnp.mean(hh * hh, axis=-1, keepdims=True) + NORM_EPS) * mhw_ref[:, vs]
        za = za_ref[:, vs].astype(F32)
        og = o_ref[:, vs].astype(F32)
        a_ref[:, vs] = ((za * _sigmoid(za)) * (_sigmoid(og) * hn)).astype(BF16)


def _mlstm(proj, gates, proj_meta, gates_meta, gbias, mhw, bsz, seq):
    nc = seq // CHUNK
    rowmap = lambda blk: (lambda b, c: (b * nc + c, blk))
    const = lambda blk: (lambda b, c: (0, blk))
    return pl.pallas_call(
        _mlstm_kernel,
        out_shape=jax.ShapeDtypeStruct((bsz * seq, M_V), BF16),
        grid=(bsz, nc),
        in_specs=[
            pl.BlockSpec((CHUNK, M_QK), rowmap(OFF_Q // M_QK)),
            pl.BlockSpec((CHUNK, M_QK), rowmap(OFF_K // M_QK)),
            pl.BlockSpec((CHUNK, M_V), rowmap(OFF_V // M_V)),
            pl.BlockSpec((CHUNK, M_V), rowmap(OFF_O // M_V)),
            pl.BlockSpec((CHUNK, M_V), rowmap(OFF_ZA // M_V)),
            pl.BlockSpec((CHUNK, GATE_PAD), rowmap(0)),
            pl.BlockSpec((N_META, M_QK), const(OFF_K // M_QK)),
            pl.BlockSpec((N_META, M_V), const(OFF_V // M_V)),
            pl.BlockSpec((N_META, GATE_PAD), const(0)),
            pl.BlockSpec((1, GATE_PAD), const(0)),
            pl.BlockSpec((1, M_V), const(0)),
        ],
        out_specs=pl.BlockSpec((CHUNK, M_V), rowmap(0)),
        scratch_shapes=[
            pltpu.VMEM((M_HEADS, M_QK_HEAD, M_V_HEAD), F32),
            pltpu.VMEM((M_HEADS, 1, M_QK_HEAD), F32),
            pltpu.VMEM((M_HEADS, 1, LANES), F32),
        ],
        compiler_params=pltpu.CompilerParams(
            dimension_semantics=("arbitrary", "arbitrary"),
            vmem_limit_bytes=VMEM_LIMIT_BYTES),
        name="mlstm",
    )(proj, proj, proj, proj, proj, gates, proj_meta, proj_meta, gates_meta, gbias, mhw)


def _merge_kernel(sh_ref, sb_ref, sc_ref, zs_ref, gab_ref, a_ref, x_ref,
                  shm_ref, scm_ref, cw_ref, wpa_ref, wpb_ref, wo_ref, fnw_ref,
                  out_ref, u_scr, hist_scr):
    tm = MERGE_TM

    @pl.when(pl.program_id(1) == 0)
    def _():
        um = scm_ref[...].astype(F32) * shm_ref[...].astype(F32)
        hist_scr[...] = um[N_META - CONV_HIST:, :]

    acc_b = jnp.zeros((tm, D_MODEL), F32)
    for cb in range(S_WIDTH // MERGE_CB):
        cs = slice(cb * MERGE_CB, (cb + 1) * MERGE_CB)
        u = sc_ref[:, cs].astype(F32) * sh_ref[:, cs].astype(F32)
        u_scr[0:CONV_HIST, :] = hist_scr[:, cs]
        u_scr[CONV_HIST:, :] = u
        hist_scr[:, cs] = u[tm - CONV_HIST:, :]
        conv = cw_ref[CONV_W - 1:CONV_W, cs] * u
        for j in range(CONV_W - 1):
            shift = CONV_W - 1 - j
            conv = conv + cw_ref[j:j + 1, cs] * u_scr[pl.ds(CONV_HIST - shift, tm), :]
        zs = zs_ref[:, cs].astype(F32)
        ys = (zs * _sigmoid(zs)) * (sb_ref[:, cs].astype(F32) * conv)
        acc_b = acc_b + jnp.dot(ys.astype(BF16), wpb_ref[cs, :], preferred_element_type=F32)

    pa = jnp.dot(a_ref[...], wpa_ref[...], preferred_element_type=F32)
    ga = gab_ref[:, 0:D_MODEL].astype(F32)
    gb = gab_ref[:, D_MODEL:2 * D_MODEL].astype(F32)
    merged = _sigmoid(ga) * pa + _sigmoid(gb) * acc_b
    y = x_ref[...] + jnp.dot(merged.astype(BF16), wo_ref[...], preferred_element_type=F32)
    ms = jnp.mean(y * y, axis=-1, keepdims=True)
    out_ref[...] = y * lax.rsqrt(ms + NORM_EPS) * fnw_ref[...]


def _merge(proj, a, x2d, proj_meta, conv_w, wpa, wpb, wo, fnw, bsz, seq):
    nt = seq // MERGE_TM
    rowmap = lambda blk: (lambda b, i: (b * nt + i, blk))
    const = lambda blk: (lambda b, i: (0, blk))
    return pl.pallas_call(
        _merge_kernel,
        out_shape=jax.ShapeDtypeStruct((bsz * seq, D_MODEL), F32),
        grid=(bsz, nt),
        in_specs=[
            pl.BlockSpec((MERGE_TM, S_WIDTH), rowmap(OFF_SH // S_WIDTH)),
            pl.BlockSpec((MERGE_TM, S_WIDTH), rowmap(OFF_SB // S_WIDTH)),
            pl.BlockSpec((MERGE_TM, S_WIDTH), rowmap(OFF_SC // S_WIDTH)),
            pl.BlockSpec((MERGE_TM, S_WIDTH), rowmap(OFF_ZS // S_WIDTH)),
            pl.BlockSpec((MERGE_TM, 2 * D_MODEL), rowmap(OFF_GA // (2 * D_MODEL))),
            pl.BlockSpec((MERGE_TM, M_V), rowmap(0)),
            pl.BlockSpec((MERGE_TM, D_MODEL), rowmap(0)),
            pl.BlockSpec((N_META, S_WIDTH), const(OFF_SH // S_WIDTH)),
            pl.BlockSpec((N_META, S_WIDTH), const(OFF_SC // S_WIDTH)),
            pl.BlockSpec((CONV_W, S_WIDTH), const(0)),
            pl.BlockSpec((M_V, D_MODEL), const(0)),
            pl.BlockSpec((S_WIDTH, D_MODEL), const(0)),
            pl.BlockSpec((D_MODEL, D_MODEL), const(0)),
            pl.BlockSpec((1, D_MODEL), const(0)),
        ],
        out_specs=pl.BlockSpec((MERGE_TM, D_MODEL), rowmap(0)),
        scratch_shapes=[
            pltpu.VMEM((MERGE_TM + CONV_HIST, MERGE_CB), F32),
            pltpu.VMEM((CONV_HIST, S_WIDTH), F32),
        ],
        compiler_params=pltpu.CompilerParams(
            dimension_semantics=("arbitrary", "arbitrary"),
            vmem_limit_bytes=VMEM_LIMIT_BYTES),
        name="merge",
    )(proj, proj, proj, proj, proj, a, x2d, proj_meta, proj_meta,
      conv_w, wpa, wpb, wo, fnw)


def kernel(x, meta_tokens, norm_w, w_in, b_igate, b_fgate, mh_norm_w, conv_w,
           w_proj_a, w_proj_b, w_out, final_norm_w):
    bsz, seq, d = x.shape
    assert d == D_MODEL and seq % PROJ_TM == 0 and seq % CHUNK == 0 and seq % MERGE_TM == 0
    assert meta_tokens.shape == (N_META, D_MODEL)
    assert norm_w.shape[0] == 1, "single-layer trunk"

    w = w_in[0]
    w_main = jnp.concatenate(
        [w[:, :OFF_GATES_IN_W], w[:, OFF_GATES_IN_W + N_GATES:]], axis=1).astype(BF16)
    w_gate = jnp.pad(w[:, OFF_GATES_IN_W:OFF_GATES_IN_W + N_GATES],
                     ((0, 0), (0, GATE_PAD - N_GATES))).astype(BF16)
    gbias = jnp.pad(jnp.concatenate([b_igate[0], b_fgate[0]]).astype(F32),
                    (0, GATE_PAD - N_GATES)).reshape(1, GATE_PAD)
    nw = norm_w[0].reshape(1, D_MODEL).astype(F32)
    mhw = mh_norm_w[0].reshape(1, M_V).astype(F32)
    fnw = final_norm_w.reshape(1, D_MODEL).astype(F32)

    x2d = x.reshape(bsz * seq, D_MODEL)
    proj, gates = _in_proj(x2d, nw, w_main, w_gate, PROJ_TM)
    proj_meta, gates_meta = _in_proj(meta_tokens.astype(x.dtype), nw, w_main, w_gate, N_META)

    a = _mlstm(proj, gates, proj_meta, gates_meta, gbias, mhw, bsz, seq)
    out = _merge(proj, a, x2d, proj_meta, conv_w[0].astype(F32),
                 w_proj_a[0].astype(BF16), w_proj_b[0].astype(BF16), w_out[0].astype(BF16),
                 fnw, bsz, seq)
    return out.reshape(bsz, seq, D_MODEL)
```

```python
import jax
import jax.numpy as jnp
from jax import lax
from jax.experimental import pallas as pl
from jax.experimental.pallas import tpu as pltpu

D_MODEL = 1024
N_META = 16
NORM_EPS = 1e-6
M_HEADS = 4
M_V = 2 * D_MODEL
M_V_HEAD = M_V // M_HEADS
M_QK_HEAD = M_V_HEAD // 2
M_QK = M_HEADS * M_QK_HEAD
GATE_SOFTCAP = 15.0
S_WIDTH = 2 * D_MODEL
CONV_W = 3

LANES = 128
GATE_PAD = LANES
N_GATES = 2 * M_HEADS
A_Q = 0
A_K = A_Q + M_QK
A_V = A_K + M_QK
A_O = A_V + M_V
A_ZA = A_O + M_V
A_GATE = A_ZA + M_V
A_COLS_IN_W = A_GATE + N_GATES
A_COLS = A_GATE + GATE_PAD
B_SH = 0
B_SB = B_SH + S_WIDTH
B_SC = B_SB + S_WIDTH
B_ZS = B_SC + S_WIDTH
B_GA = B_ZS + S_WIDTH
B_GB = B_GA + D_MODEL
B_COLS = B_GB + D_MODEL

VMEM_LIMIT_BYTES = 56 * 1024 * 1024

TILE = 256
CONV_CB = 512
CONV_HIST = 8

BF16 = jnp.bfloat16
F32 = jnp.float32


def _sigmoid(x):
    return 1.0 / (1.0 + jnp.exp(-x))


def _rmsnorm(x, w):
    ms = jnp.mean(x * x, axis=-1, keepdims=True)
    return x * lax.rsqrt(ms + NORM_EPS) * w


def _dot(a, b):
    return jnp.dot(a, b, preferred_element_type=F32)


def _resident(shape):
    return pl.BlockSpec(shape, lambda b, i: (0,) * len(shape), pipeline_mode=pl.Buffered(1))


def _split3(x):
    hi = x.astype(BF16)
    r1 = x - hi.astype(F32)
    mid = r1.astype(BF16)
    lo = (r1 - mid.astype(F32)).astype(BF16)
    return hi, mid, lo


def _log_gates(g, bias):
    z = GATE_SOFTCAP * jnp.tanh((g + bias) / GATE_SOFTCAP)
    log_f = jnp.minimum(z, 0.0) - jnp.log1p(jnp.exp(-jnp.abs(z)))
    lane = lax.broadcasted_iota(jnp.int32, g.shape, 1)
    return jnp.where(lane < M_HEADS, z, jnp.where(lane < N_GATES, log_f, 0.0))


def _cumsum_rows(a):
    t = a.shape[0]
    r = lax.broadcasted_iota(jnp.int32, (t, t), 0)
    c = lax.broadcasted_iota(jnp.int32, (t, t), 1)
    tri = jnp.where(r >= c, 1.0, 0.0).astype(BF16)
    hi, mid, lo = _split3(a)
    return _dot(tri, lo) + _dot(tri, mid) + _dot(tri, hi)


def _state_update(c_prev, n_prev, m_prev, k, v, li_col, b_col, b_last):
    wlog = b_last - b_col + li_col
    m_new = jnp.maximum(b_last + m_prev, jnp.max(wlog, axis=0, keepdims=True))
    decay = jnp.exp(b_last + m_prev - m_new)
    kw = jnp.exp(wlog - m_new) * k
    c_new = decay * c_prev + lax.dot_general(
        kw.astype(BF16), v, (((0,), (0,)), ((), ())), preferred_element_type=F32)
    n_new = decay * n_prev + jnp.sum(kw, axis=0, keepdims=True)
    return c_new, n_new, m_new


def _mlstm_side_kernel(x_ref, meta_ref, nw_ref, wa_ref, wpa_ref, gbias_ref, mhw_ref,
                       pa_ref, c_scr, n_scr, m_scr):
    t = TILE
    gbias = gbias_ref[...]
    nw = nw_ref[...]

    def proj(xn, off, width):
        return _dot(xn, wa_ref[:, off:off + width])

    @pl.when(pl.program_id(1) == 0)
    def _():
        xm = _rmsnorm(meta_ref[...], nw).astype(BF16)
        la = _log_gates(proj(xm, A_GATE, GATE_PAD), gbias)
        cum = _cumsum_rows(la)
        for h in range(M_HEADS):
            li_col = la[:, h:h + 1]
            b_col = cum[:, M_HEADS + h:M_HEADS + h + 1]
            b_last = b_col[N_META - 1:N_META, :]
            k = proj(xm, A_K + h * M_QK_HEAD, M_QK_HEAD)
            v = proj(xm, A_V + h * M_V_HEAD, M_V_HEAD).astype(BF16)
            c_new, n_new, m_new = _state_update(
                jnp.zeros((M_QK_HEAD, M_V_HEAD), F32), jnp.zeros((1, M_QK_HEAD), F32),
                jnp.zeros((1, 1), F32), k, v, li_col, b_col, b_last)
            c_scr[h] = c_new
            n_scr[h] = n_new
            m_scr[h] = jnp.broadcast_to(m_new, (1, LANES))

    xn = _rmsnorm(x_ref[...], nw).astype(BF16)
    la = _log_gates(proj(xn, A_GATE, GATE_PAD), gbias)
    cum = _cumsum_rows(la)
    la_t = la.T
    cum_t = cum.T
    row = lax.broadcasted_iota(jnp.int32, (t, t), 0)
    col = lax.broadcasted_iota(jnp.int32, (t, t), 1)
    causal = row >= col

    pa = jnp.zeros((t, D_MODEL), F32)
    for h in range(M_HEADS):
        vs = slice(h * M_V_HEAD, (h + 1) * M_V_HEAD)
        q = (proj(xn, A_Q + h * M_QK_HEAD, M_QK_HEAD) * (M_QK_HEAD ** -0.5)).astype(BF16)
        k = proj(xn, A_K + h * M_QK_HEAD, M_QK_HEAD)
        v = proj(xn, A_V + h * M_V_HEAD, M_V_HEAD).astype(BF16)
        c_prev = c_scr[h]
        n_prev = n_scr[h]
        m_prev = m_scr[h][:, 0:1]

        li_col = la[:, h:h + 1]
        b_col = cum[:, M_HEADS + h:M_HEADS + h + 1]
        li_row = la_t[h:h + 1, :]
        b_row = cum_t[M_HEADS + h:M_HEADS + h + 1, :]
        b_last = b_col[t - 1:t, :]

        s = lax.dot_general(q, k.astype(BF16), (((1,), (1,)), ((), ())),
                            preferred_element_type=F32)
        dlog = jnp.where(causal, b_col - b_row + li_row, -jnp.inf)
        inter = b_col + m_prev
        m_t = jnp.maximum(inter, jnp.max(dlog, axis=-1, keepdims=True))
        w = jnp.exp(dlog - m_t) * s
        wi = jnp.exp(inter - m_t)
        num = wi * _dot(q, c_prev.astype(BF16)) + _dot(w.astype(BF16), v)
        qn = jnp.sum(q.astype(F32) * n_prev, axis=-1, keepdims=True)
        den = wi * qn + jnp.sum(w, axis=-1, keepdims=True)
        hh = num / jnp.maximum(jnp.abs(den), jnp.exp(-m_t))

        c_new, n_new, m_new = _state_update(c_prev, n_prev, m_prev, k, v, li_col, b_col, b_last)
        c_scr[h] = c_new
        n_scr[h] = n_new
        m_scr[h] = jnp.broadcast_to(m_new, (1, LANES))

        hn = hh * lax.rsqrt(jnp.mean(hh * hh, axis=-1, keepdims=True) + NORM_EPS) * mhw_ref[:, vs]
        og = proj(xn, A_O + h * M_V_HEAD, M_V_HEAD)
        za = proj(xn, A_ZA + h * M_V_HEAD, M_V_HEAD)
        a = (za * _sigmoid(za)) * (_sigmoid(og) * hn)
        pa = pa + _dot(a.astype(BF16), wpa_ref[vs, :])
    pa_ref[...] = pa


def _mlstm_side(x2d, meta, nw, wa, wpa, gbias, mhw, bsz, seq):
    nt = seq // TILE
    rows = lambda b, i: (b * nt + i, 0)
    return pl.pallas_call(
        _mlstm_side_kernel,
        out_shape=jax.ShapeDtypeStruct((bsz * seq, D_MODEL), F32),
        grid=(bsz, nt),
        in_specs=[
            pl.BlockSpec((TILE, D_MODEL), rows),
            _resident((N_META, D_MODEL)),
            _resident((1, D_MODEL)),
            _resident((D_MODEL, A_COLS)),
            _resident((M_V, D_MODEL)),
            _resident((1, GATE_PAD)),
            _resident((1, M_V)),
        ],
        out_specs=pl.BlockSpec((TILE, D_MODEL), rows),
        scratch_shapes=[
            pltpu.VMEM((M_HEADS, M_QK_HEAD, M_V_HEAD), F32),
            pltpu.VMEM((M_HEADS, 1, M_QK_HEAD), F32),
            pltpu.VMEM((M_HEADS, 1, LANES), F32),
        ],
        compiler_params=pltpu.CompilerParams(
            dimension_semantics=("arbitrary", "arbitrary"),
            vmem_limit_bytes=VMEM_LIMIT_BYTES),
        name="mlstm_side",
    )(x2d, meta, nw, wa, wpa, gbias, mhw)


def _conv_side_kernel(x_ref, pa_ref, meta_ref, nw_ref, wb_ref, cw_ref, wpb_ref, wo_ref, fnw_ref,
                      out_ref, u_scr, hist_scr):
    t = TILE
    nw = nw_ref[...]

    def proj(xn, off, width):
        return _dot(xn, wb_ref[:, off:off + width])

    @pl.when(pl.program_id(1) == 0)
    def _():
        xm = _rmsnorm(meta_ref[...], nw).astype(BF16)
        um = proj(xm, B_SC, S_WIDTH) * proj(xm, B_SH, S_WIDTH)
        hist_scr[...] = um[N_META - CONV_HIST:, :]

    x = x_ref[...]
    xn = _rmsnorm(x, nw).astype(BF16)
    acc_b = jnp.zeros((t, D_MODEL), F32)
    for cb in range(S_WIDTH // CONV_CB):
        cs = slice(cb * CONV_CB, (cb + 1) * CONV_CB)
        u = proj(xn, B_SC + cb * CONV_CB, CONV_CB) * proj(xn, B_SH + cb * CONV_CB, CONV_CB)
        u_scr[0:CONV_HIST, :] = hist_scr[:, cs]
        u_scr[CONV_HIST:, :] = u
        hist_scr[:, cs] = u[t - CONV_HIST:, :]
        conv = cw_ref[CONV_W - 1:CONV_W, cs] * u
        for j in range(CONV_W - 1):
            shift = CONV_W - 1 - j
            conv = conv + cw_ref[j:j + 1, cs] * u_scr[pl.ds(CONV_HIST - shift, t), :]
        zs = proj(xn, B_ZS + cb * CONV_CB, CONV_CB)
        sb = proj(xn, B_SB + cb * CONV_CB, CONV_CB)
        ys = (zs * _sigmoid(zs)) * (sb * conv)
        acc_b = acc_b + _dot(ys.astype(BF16), wpb_ref[cs, :])

    ga = proj(xn, B_GA, D_MODEL)
    gb = proj(xn, B_GB, D_MODEL)
    merged = _sigmoid(ga) * pa_ref[...] + _sigmoid(gb) * acc_b
    y = x + _dot(merged.astype(BF16), wo_ref[...])
    out_ref[...] = _rmsnorm(y, fnw_ref[...])


def _conv_side(x2d, pa, meta, nw, wb, conv_w, wpb, wo, fnw, bsz, seq):
    nt = seq // TILE
    rows = lambda b, i: (b * nt + i, 0)
    return pl.pallas_call(
        _conv_side_kernel,
        out_shape=jax.ShapeDtypeStruct((bsz * seq, D_MODEL), F32),
        grid=(bsz, nt),
        in_specs=[
            pl.BlockSpec((TILE, D_MODEL), rows),
            pl.BlockSpec((TILE, D_MODEL), rows),
            _resident((N_META, D_MODEL)),
            _resident((1, D_MODEL)),
            _resident((D_MODEL, B_COLS)),
            _resident((CONV_W, S_WIDTH)),
            _resident((S_WIDTH, D_MODEL)),
            _resident((D_MODEL, D_MODEL)),
            _resident((1, D_MODEL)),
        ],
        out_specs=pl.BlockSpec((TILE, D_MODEL), rows),
        scratch_shapes=[
            pltpu.VMEM((TILE + CONV_HIST, CONV_CB), F32),
            pltpu.VMEM((CONV_HIST, S_WIDTH), F32),
        ],
        compiler_params=pltpu.CompilerParams(
            dimension_semantics=("arbitrary", "arbitrary"),
            vmem_limit_bytes=VMEM_LIMIT_BYTES),
        name="conv_side",
    )(x2d, pa, meta, nw, wb, conv_w, wpb, wo, fnw)


def kernel(x, meta_tokens, norm_w, w_in, b_igate, b_fgate, mh_norm_w, conv_w,
           w_proj_a, w_proj_b, w_out, final_norm_w):
    bsz, seq, d = x.shape
    assert d == D_MODEL and seq % TILE == 0
    assert meta_tokens.shape == (N_META, D_MODEL)
    assert norm_w.shape[0] == 1, "single-layer trunk"
    assert w_in.shape[-1] == A_COLS_IN_W + B_COLS

    w = w_in[0]
    wa = jnp.pad(w[:, :A_COLS_IN_W], ((0, 0), (0, A_COLS - A_COLS_IN_W))).astype(BF16)
    wb = w[:, A_COLS_IN_W:].astype(BF16)
    gbias = jnp.pad(jnp.concatenate([b_igate[0], b_fgate[0]]).astype(F32),
                    (0, GATE_PAD - N_GATES)).reshape(1, GATE_PAD)
    nw = norm_w[0].reshape(1, D_MODEL).astype(F32)
    mhw = mh_norm_w[0].reshape(1, M_V).astype(F32)
    fnw = final_norm_w.reshape(1, D_MODEL).astype(F32)
    meta = meta_tokens.astype(F32)

    x2d = x.reshape(bsz * seq, D_MODEL)
    pa = _mlstm_side(x2d, meta, nw, wa, w_proj_a[0].astype(BF16), gbias, mhw, bsz, seq)
    out = _conv_side(x2d, pa, meta, nw, wb, conv_w[0].astype(F32),
                     w_proj_b[0].astype(BF16), w_out[0].astype(BF16), fnw, bsz, seq)
    return out.reshape(bsz, seq, D_MODEL)
```

```python
import jax
import jax.numpy as jnp
from jax import lax
from jax.experimental import pallas as pl
from jax.experimental.pallas import tpu as pltpu

D_MODEL = 1024
N_META = 16
NORM_EPS = 1e-6
M_HEADS = 4
M_V = 2 * D_MODEL
M_V_HEAD = M_V // M_HEADS
M_QK_HEAD = M_V_HEAD // 2
M_QK = M_HEADS * M_QK_HEAD
GATE_SOFTCAP = 15.0
S_WIDTH = 2 * D_MODEL
CONV_W = 3

LANES = 128
GATE_PAD = LANES
N_GATES = 2 * M_HEADS
A_Q = 0
A_K = A_Q + M_QK
A_V = A_K + M_QK
A_O = A_V + M_V
A_ZA = A_O + M_V
A_GATE = A_ZA + M_V
A_COLS_IN_W = A_GATE + N_GATES
A_COLS = A_GATE + GATE_PAD
B_SH = 0
B_SB = B_SH + S_WIDTH
B_SC = B_SB + S_WIDTH
B_ZS = B_SC + S_WIDTH
B_GA = B_ZS + S_WIDTH
B_GB = B_GA + D_MODEL
B_COLS = B_GB + D_MODEL

VMEM_LIMIT_BYTES = 56 * 1024 * 1024

TILE = 256
MLSTM_NB = 1
CONV_TILE = 512
CONV_CB = 512
CONV_HIST = 8
PACK_ROWS = 256
PACK_A_COLS = 1664
PACK_B_COLS = 1024

BF16 = jnp.bfloat16
F32 = jnp.float32


def _sigmoid(x):
    return 1.0 / (1.0 + jnp.exp(-x))


def _rmsnorm(x, w):
    ms = jnp.mean(x * x, axis=-1, keepdims=True)
    return x * lax.rsqrt(ms + NORM_EPS) * w


def _dot(a, b):
    return jnp.dot(a, b, preferred_element_type=F32)


def _resident(shape):
    return pl.BlockSpec(shape, lambda b, i: (0,) * len(shape), pipeline_mode=pl.Buffered(1))


def _pack_wa_kernel(w_ref, o_ref):
    o_ref[...] = w_ref[...].astype(BF16)


def _pack_wb_kernel(w_ref, nxt_ref, o_ref):
    cat = jnp.concatenate([w_ref[...], nxt_ref[...]], axis=1)
    cat = pltpu.roll(cat, cat.shape[1] - N_GATES, 1)
    o_ref[...] = cat[:, :PACK_B_COLS].astype(BF16)


def _pack_weights(w):
    params = pltpu.CompilerParams(dimension_semantics=("arbitrary", "arbitrary"))
    wa = pl.pallas_call(
        _pack_wa_kernel,
        out_shape=jax.ShapeDtypeStruct((D_MODEL, A_COLS), BF16),
        grid=(D_MODEL // PACK_ROWS, A_COLS // PACK_A_COLS),
        in_specs=[pl.BlockSpec((PACK_ROWS, PACK_A_COLS), lambda r, c: (r, c))],
        out_specs=pl.BlockSpec((PACK_ROWS, PACK_A_COLS), lambda r, c: (r, c)),
        compiler_params=params,
        name="pack_wa",
    )(w)
    first = A_GATE // PACK_B_COLS
    lanes_per_blk = PACK_B_COLS // LANES
    wb = pl.pallas_call(
        _pack_wb_kernel,
        out_shape=jax.ShapeDtypeStruct((D_MODEL, B_COLS), BF16),
        grid=(D_MODEL // PACK_ROWS, B_COLS // PACK_B_COLS),
        in_specs=[
            pl.BlockSpec((PACK_ROWS, PACK_B_COLS), lambda r, c: (r, first + c)),
            pl.BlockSpec((PACK_ROWS, LANES), lambda r, c: (r, (first + c + 1) * lanes_per_blk)),
        ],
        out_specs=pl.BlockSpec((PACK_ROWS, PACK_B_COLS), lambda r, c: (r, c)),
        compiler_params=params,
        name="pack_wb",
    )(w, w)
    return wa, wb


def _split3(x):
    hi = x.astype(BF16)
    r1 = x - hi.astype(F32)
    mid = r1.astype(BF16)
    lo = (r1 - mid.astype(F32)).astype(BF16)
    return hi, mid, lo


def _log_gates(g, bias):
    z = GATE_SOFTCAP * jnp.tanh((g + bias) / GATE_SOFTCAP)
    log_f = jnp.minimum(z, 0.0) - jnp.log1p(jnp.exp(-jnp.abs(z)))
    lane = lax.broadcasted_iota(jnp.int32, g.shape, 1)
    return jnp.where(lane < M_HEADS, z, jnp.where(lane < N_GATES, log_f, 0.0))


def _cumsum_rows(a):
    t = a.shape[0]
    r = lax.broadcasted_iota(jnp.int32, (t, t), 0)
    c = lax.broadcasted_iota(jnp.int32, (t, t), 1)
    tri = jnp.where(r >= c, 1.0, 0.0).astype(BF16)
    hi, mid, lo = _split3(a)
    return _dot(tri, lo) + _dot(tri, mid) + _dot(tri, hi)


def _state_update(c_prev, n_prev, m_prev, k, v, li_col, b_col, b_last):
    wlog = b_last - b_col + li_col
    m_new = jnp.maximum(b_last + m_prev, jnp.max(wlog, axis=0, keepdims=True))
    decay = jnp.exp(b_last + m_prev - m_new)
    kw = jnp.exp(wlog - m_new) * k
    c_new = decay * c_prev + lax.dot_general(
        kw.astype(BF16), v, (((0,), (0,)), ((), ())), preferred_element_type=F32)
    n_new = decay * n_prev + jnp.sum(kw, axis=0, keepdims=True)
    return c_new, n_new, m_new


def _mlstm_side_kernel(x_ref, meta_ref, nw_ref, wa_ref, wpa_ref, gbias_ref, mhw_ref,
                       pa_ref, c_scr, n_scr, m_scr):
    t = TILE
    nb = MLSTM_NB
    gbias = gbias_ref[...]
    nw = nw_ref[...]

    def proj(xn, off, width):
        return _dot(xn, wa_ref[:, off:off + width])

    @pl.when(pl.program_id(1) == 0)
    def _():
        xm = _rmsnorm(meta_ref[...], nw).astype(BF16)
        la = _log_gates(proj(xm, A_GATE, GATE_PAD), gbias)
        cum = _cumsum_rows(la)
        for h in range(M_HEADS):
            li_col = la[:, h:h + 1]
            b_col = cum[:, M_HEADS + h:M_HEADS + h + 1]
            b_last = b_col[N_META - 1:N_META, :]
            k = proj(xm, A_K + h * M_QK_HEAD, M_QK_HEAD)
            v = proj(xm, A_V + h * M_V_HEAD, M_V_HEAD).astype(BF16)
            c_new, n_new, m_new = _state_update(
                jnp.zeros((M_QK_HEAD, M_V_HEAD), F32), jnp.zeros((1, M_QK_HEAD), F32),
                jnp.zeros((1, 1), F32), k, v, li_col, b_col, b_last)
            for b in range(nb):
                c_scr[b * M_HEADS + h] = c_new
                n_scr[b * M_HEADS + h] = n_new
                m_scr[b * M_HEADS + h] = jnp.broadcast_to(m_new, (1, LANES))

    xn = _rmsnorm(x_ref[...].reshape(nb * t, D_MODEL), nw).astype(BF16)
    la_all = _log_gates(proj(xn, A_GATE, GATE_PAD), gbias)
    la, cum, la_t, cum_t = [], [], [], []
    for b in range(nb):
        la.append(la_all[b * t:(b + 1) * t, :])
        cum.append(_cumsum_rows(la[b]))
        la_t.append(la[b].T)
        cum_t.append(cum[b].T)
    row = lax.broadcasted_iota(jnp.int32, (t, t), 0)
    col = lax.broadcasted_iota(jnp.int32, (t, t), 1)
    causal = row >= col

    pa = jnp.zeros((nb * t, D_MODEL), F32)
    for h in range(M_HEADS):
        vs = slice(h * M_V_HEAD, (h + 1) * M_V_HEAD)
        q_all = (proj(xn, A_Q + h * M_QK_HEAD, M_QK_HEAD) * (M_QK_HEAD ** -0.5)).astype(BF16)
        k_all = proj(xn, A_K + h * M_QK_HEAD, M_QK_HEAD)
        v_all = proj(xn, A_V + h * M_V_HEAD, M_V_HEAD).astype(BF16)
        a_parts = []
        for b in range(nb):
            rs = slice(b * t, (b + 1) * t)
            slot = b * M_HEADS + h
            q, k, v = q_all[rs], k_all[rs], v_all[rs]
            c_prev = c_scr[slot]
            n_prev = n_scr[slot]
            m_prev = m_scr[slot][:, 0:1]

            li_col = la[b][:, h:h + 1]
            b_col = cum[b][:, M_HEADS + h:M_HEADS + h + 1]
            li_row = la_t[b][h:h + 1, :]
            b_row = cum_t[b][M_HEADS + h:M_HEADS + h + 1, :]
            b_last = b_col[t - 1:t, :]

            s = lax.dot_general(q, k.astype(BF16), (((1,), (1,)), ((), ())),
                                preferred_element_type=F32)
            dlog = jnp.where(causal, b_col - b_row + li_row, -jnp.inf)
            inter = b_col + m_prev
            m_t = jnp.maximum(inter, jnp.max(dlog, axis=-1, keepdims=True))
            w = jnp.exp(dlog - m_t) * s
            wi = jnp.exp(inter - m_t)
            num = wi * _dot(q, c_prev.astype(BF16)) + _dot(w.astype(BF16), v)
            qn = jnp.sum(q.astype(F32) * n_prev, axis=-1, keepdims=True)
            den = wi * qn + jnp.sum(w, axis=-1, keepdims=True)
            hh = num / jnp.maximum(jnp.abs(den), jnp.exp(-m_t))

            c_new, n_new, m_new = _state_update(c_prev, n_prev, m_prev, k, v,
                                                li_col, b_col, b_last)
            c_scr[slot] = c_new
            n_scr[slot] = n_new
            m_scr[slot] = jnp.broadcast_to(m_new, (1, LANES))

            hn = (hh * lax.rsqrt(jnp.mean(hh * hh, axis=-1, keepdims=True) + NORM_EPS)
                  * mhw_ref[:, vs])
            if b == 0:
                og_all = proj(xn, A_O + h * M_V_HEAD, M_V_HEAD)
                za_all = proj(xn, A_ZA + h * M_V_HEAD, M_V_HEAD)
            za = za_all[rs]
            a = (za * _sigmoid(za)) * (_sigmoid(og_all[rs]) * hn)
            a_parts.append(a.astype(BF16))
        pa = pa + _dot(jnp.concatenate(a_parts, axis=0), wpa_ref[vs, :])
    pa_ref[...] = pa.reshape(nb, t, D_MODEL)


def _mlstm_side(x, meta, nw, wa, wpa, gbias, mhw):
    bsz, seq, _ = x.shape
    rows = lambda g, i: (g, i, 0)
    return pl.pallas_call(
        _mlstm_side_kernel,
        out_shape=jax.ShapeDtypeStruct((bsz, seq, D_MODEL), F32),
        grid=(bsz // MLSTM_NB, seq // TILE),
        in_specs=[
            pl.BlockSpec((MLSTM_NB, TILE, D_MODEL), rows),
            _resident((N_META, D_MODEL)),
            _resident((1, D_MODEL)),
            _resident((D_MODEL, A_COLS)),
            _resident((M_V, D_MODEL)),
            _resident((1, GATE_PAD)),
            _resident((1, M_V)),
        ],
        out_specs=pl.BlockSpec((MLSTM_NB, TILE, D_MODEL), rows),
        scratch_shapes=[
            pltpu.VMEM((MLSTM_NB * M_HEADS, M_QK_HEAD, M_V_HEAD), F32),
            pltpu.VMEM((MLSTM_NB * M_HEADS, 1, M_QK_HEAD), F32),
            pltpu.VMEM((MLSTM_NB * M_HEADS, 1, LANES), F32),
        ],
        compiler_params=pltpu.CompilerParams(
            dimension_semantics=("arbitrary", "arbitrary"),
            vmem_limit_bytes=VMEM_LIMIT_BYTES),
        name="mlstm_side",
    )(x, meta, nw, wa, wpa, gbias, mhw)


def _conv_side_kernel(x_ref, pa_ref, meta_ref, nw_ref, wb_ref, cw_ref, wpb_ref, wo_ref, fnw_ref,
                      out_ref, u_scr, hist_scr):
    t = CONV_TILE
    nw = nw_ref[...]

    def proj(xn, off, width):
        return _dot(xn, wb_ref[:, off:off + width])

    @pl.when(pl.program_id(1) == 0)
    def _():
        xm = _rmsnorm(meta_ref[...], nw).astype(BF16)
        um = proj(xm, B_SC, S_WIDTH) * proj(xm, B_SH, S_WIDTH)
        hist_scr[...] = um[N_META - CONV_HIST:, :]

    x = x_ref[...]
    xn = _rmsnorm(x, nw).astype(BF16)
    acc_b = jnp.zeros((t, D_MODEL), F32)
    for cb in range(S_WIDTH // CONV_CB):
        cs = slice(cb * CONV_CB, (cb + 1) * CONV_CB)
        u = proj(xn, B_SC + cb * CONV_CB, CONV_CB) * proj(xn, B_SH + cb * CONV_CB, CONV_CB)
        u_scr[0:CONV_HIST, :] = hist_scr[:, cs]
        u_scr[CONV_HIST:, :] = u
        hist_scr[:, cs] = u[t - CONV_HIST:, :]
        conv = cw_ref[CONV_W - 1:CONV_W, cs] * u
        for j in range(CONV_W - 1):
            shift = CONV_W - 1 - j
            conv = conv + cw_ref[j:j + 1, cs] * u_scr[pl.ds(CONV_HIST - shift, t), :]
        zs = proj(xn, B_ZS + cb * CONV_CB, CONV_CB)
        sb = proj(xn, B_SB + cb * CONV_CB, CONV_CB)
        ys = (zs * _sigmoid(zs)) * (sb * conv)
        acc_b = acc_b + _dot(ys.astype(BF16), wpb_ref[cs, :])

    ga = proj(xn, B_GA, D_MODEL)
    gb = proj(xn, B_GB, D_MODEL)
    merged = _sigmoid(ga) * pa_ref[...] + _sigmoid(gb) * acc_b
    y = x + _dot(merged.astype(BF16), wo_ref[...])
    out_ref[...] = _rmsnorm(y, fnw_ref[...])


def _conv_side(x2d, pa, meta, nw, wb, conv_w, wpb, wo, fnw, bsz, seq):
    nt = seq // CONV_TILE
    rows = lambda b, i: (b * nt + i, 0)
    return pl.pallas_call(
        _conv_side_kernel,
        out_shape=jax.ShapeDtypeStruct((bsz * seq, D_MODEL), F32),
        grid=(bsz, nt),
        in_specs=[
            pl.BlockSpec((CONV_TILE, D_MODEL), rows),
            pl.BlockSpec((CONV_TILE, D_MODEL), rows),
            _resident((N_META, D_MODEL)),
            _resident((1, D_MODEL)),
            _resident((D_MODEL, B_COLS)),
            _resident((CONV_W, S_WIDTH)),
            _resident((S_WIDTH, D_MODEL)),
            _resident((D_MODEL, D_MODEL)),
            _resident((1, D_MODEL)),
        ],
        out_specs=pl.BlockSpec((CONV_TILE, D_MODEL), rows),
        scratch_shapes=[
            pltpu.VMEM((CONV_TILE + CONV_HIST, CONV_CB), F32),
            pltpu.VMEM((CONV_HIST, S_WIDTH), F32),
        ],
        compiler_params=pltpu.CompilerParams(
            dimension_semantics=("arbitrary", "arbitrary"),
            vmem_limit_bytes=VMEM_LIMIT_BYTES),
        name="conv_side",
    )(x2d, pa, meta, nw, wb, conv_w, wpb, wo, fnw)


def kernel(x, meta_tokens, norm_w, w_in, b_igate, b_fgate, mh_norm_w, conv_w,
           w_proj_a, w_proj_b, w_out, final_norm_w):
    bsz, seq, d = x.shape
    assert d == D_MODEL and seq % TILE == 0 and seq % CONV_TILE == 0 and bsz % MLSTM_NB == 0
    assert meta_tokens.shape == (N_META, D_MODEL)
    assert norm_w.shape[0] == 1, "single-layer trunk"
    assert w_in.shape[-1] == A_COLS_IN_W + B_COLS
    assert A_COLS % PACK_A_COLS == 0 and A_GATE % PACK_B_COLS == 0 and B_COLS % PACK_B_COLS == 0

    wa, wb = _pack_weights(w_in[0])
    gbias = jnp.pad(jnp.concatenate([b_igate[0], b_fgate[0]]).astype(F32),
                    (0, GATE_PAD - N_GATES)).reshape(1, GATE_PAD)
    nw = norm_w[0].reshape(1, D_MODEL).astype(F32)
    mhw = mh_norm_w[0].reshape(1, M_V).astype(F32)
    fnw = final_norm_w.reshape(1, D_MODEL).astype(F32)
    meta = meta_tokens.astype(F32)

    pa = _mlstm_side(x, meta, nw, wa, w_proj_a[0].astype(BF16), gbias, mhw)
    out = _conv_side(x.reshape(bsz * seq, D_MODEL), pa.reshape(bsz * seq, D_MODEL), meta, nw, wb,
                     conv_w[0].astype(F32), w_proj_b[0].astype(BF16), w_out[0].astype(BF16),
                     fnw, bsz, seq)
    return out.reshape(bsz, seq, D_MODEL)
```

```python
import jax
import jax.numpy as jnp
from jax import lax
from jax.experimental import pallas as pl
from jax.experimental.pallas import tpu as pltpu

D_MODEL = 1024
N_META = 16
NORM_EPS = 1e-6
M_HEADS = 4
M_V = 2 * D_MODEL
M_V_HEAD = M_V // M_HEADS
M_QK_HEAD = M_V_HEAD // 2
M_QK = M_HEADS * M_QK_HEAD
GATE_SOFTCAP = 15.0
S_WIDTH = 2 * D_MODEL
CONV_W = 3

LANES = 128
GATE_PAD = LANES
N_GATES = 2 * M_HEADS
A_Q = 0
A_K = A_Q + M_QK
A_V = A_K + M_QK
A_O = A_V + M_V
A_ZA = A_O + M_V
A_GATE = A_ZA + M_V
A_COLS_IN_W = A_GATE + N_GATES
A_COLS = A_GATE + GATE_PAD
B_SH = 0
B_SB = B_SH + S_WIDTH
B_SC = B_SB + S_WIDTH
B_ZS = B_SC + S_WIDTH
B_GA = B_ZS + S_WIDTH
B_GB = B_GA + D_MODEL
B_COLS = B_GB + D_MODEL

VMEM_LIMIT_BYTES = 56 * 1024 * 1024

CHUNK = 256
MLSTM_ROWS = 512
CONV_TILE = 512
CONV_CB = 512
CONV_HIST = 8
PACK_A_COLS = 1664
PACK_B_COLS = 1024
PACK_SUB = 128
F32_SUBLANES = 8

BF16 = jnp.bfloat16
F32 = jnp.float32


def _sigmoid(x):
    return 1.0 / (1.0 + jnp.exp(-x))


def _rmsnorm(x, w):
    ms = jnp.mean(x * x, axis=-1, keepdims=True)
    return x * lax.rsqrt(ms + NORM_EPS) * w


def _dot(a, b):
    return jnp.dot(a, b, preferred_element_type=F32)


def _resident(shape):
    return pl.BlockSpec(shape, lambda b, i: (0,) * len(shape), pipeline_mode=pl.Buffered(1))


def _pack_wa_kernel(wt_ref, o_ref):
    for r in range(0, PACK_A_COLS, PACK_SUB):
        o_ref[:, r:r + PACK_SUB] = wt_ref[r:r + PACK_SUB, :].T.astype(BF16)


def _pack_wb_kernel(wt_ref, nxt_ref, o_ref):
    for r in range(0, PACK_B_COLS, PACK_SUB):
        if r + PACK_SUB < PACK_B_COLS:
            blk = wt_ref[N_GATES + r:N_GATES + r + PACK_SUB, :]
        else:
            blk = jnp.concatenate([wt_ref[N_GATES + r:, :], nxt_ref[...]], axis=0)
        o_ref[:, r:r + PACK_SUB] = blk.T.astype(BF16)


def _pack_weights(wt):
    params = pltpu.CompilerParams(dimension_semantics=("arbitrary",),
                                  vmem_limit_bytes=VMEM_LIMIT_BYTES)
    wa = pl.pallas_call(
        _pack_wa_kernel,
        out_shape=jax.ShapeDtypeStruct((D_MODEL, A_COLS), BF16),
        grid=(A_COLS // PACK_A_COLS,),
        in_specs=[pl.BlockSpec((PACK_A_COLS, D_MODEL), lambda c: (c, 0))],
        out_specs=pl.BlockSpec((D_MODEL, PACK_A_COLS), lambda c: (0, c)),
        compiler_params=params,
        name="pack_wa",
    )(wt)
    first = A_GATE // PACK_B_COLS
    wb = pl.pallas_call(
        _pack_wb_kernel,
        out_shape=jax.ShapeDtypeStruct((D_MODEL, B_COLS), BF16),
        grid=(B_COLS // PACK_B_COLS,),
        in_specs=[
            pl.BlockSpec((PACK_B_COLS, D_MODEL), lambda c: (first + c, 0)),
            pl.BlockSpec((N_GATES, D_MODEL),
                         lambda c: ((A_GATE + (c + 1) * PACK_B_COLS) // N_GATES, 0)),
        ],
        out_specs=pl.BlockSpec((D_MODEL, PACK_B_COLS), lambda c: (0, c)),
        compiler_params=params,
        name="pack_wb",
    )(wt, wt)
    return wa, wb


def _split3(x):
    hi = x.astype(BF16)
    r1 = x - hi.astype(F32)
    mid = r1.astype(BF16)
    lo = (r1 - mid.astype(F32)).astype(BF16)
    return hi, mid, lo


def _log_gates(g, bias):
    z = GATE_SOFTCAP * jnp.tanh((g + bias) / GATE_SOFTCAP)
    log_f = jnp.minimum(z, 0.0) - jnp.log1p(jnp.exp(-jnp.abs(z)))
    lane = lax.broadcasted_iota(jnp.int32, g.shape, 1)
    return jnp.where(lane < M_HEADS, z, jnp.where(lane < N_GATES, log_f, 0.0))


def _lower_tri(t):
    r = lax.broadcasted_iota(jnp.int32, (t, t), 0)
    c = lax.broadcasted_iota(jnp.int32, (t, t), 1)
    return jnp.where(r >= c, 1.0, 0.0).astype(BF16)


def _cumsum_rows(a, tri):
    hi, mid, lo = _split3(a)
    return _dot(tri, lo) + _dot(tri, mid) + _dot(tri, hi)


def _state_update(c_prev, n_prev, m_prev, k, v, li_col, b_col, b_last):
    wlog = b_last - b_col + li_col
    m_new = jnp.maximum(b_last + m_prev, jnp.max(wlog, axis=0, keepdims=True))
    decay = jnp.exp(b_last + m_prev - m_new)
    kw = jnp.exp(wlog - m_new) * k
    c_new = decay * c_prev + lax.dot_general(
        kw.astype(BF16), v, (((0,), (0,)), ((), ())), preferred_element_type=F32)
    n_new = decay * n_prev + jnp.sum(kw, axis=0, keepdims=True)
    return c_new, n_new, m_new


def _mlstm_side_kernel(x_ref, meta_ref, nw_ref, wa_ref, wpa_ref, gbias_ref, mhw_ref,
                       pa_ref, xn_ref, c_scr, n_scr, m_scr, la_scr, cum_scr, lat_scr, cumt_scr):
    t = CHUNK
    n_chunks = MLSTM_ROWS // t
    gbias = gbias_ref[...]
    nw = nw_ref[...]

    def proj(xn, off, width):
        return _dot(xn, wa_ref[:, off:off + width])

    @pl.when(pl.program_id(1) == 0)
    def _():
        xm = _rmsnorm(meta_ref[...], nw).astype(BF16)
        la = _log_gates(proj(xm, A_GATE, GATE_PAD), gbias)
        cum = _cumsum_rows(la, _lower_tri(N_META))
        for h in range(M_HEADS):
            li_col = la[:, h:h + 1]
            b_col = cum[:, M_HEADS + h:M_HEADS + h + 1]
            b_last = b_col[N_META - 1:N_META, :]
            k = proj(xm, A_K + h * M_QK_HEAD, M_QK_HEAD)
            v = proj(xm, A_V + h * M_V_HEAD, M_V_HEAD).astype(BF16)
            c_new, n_new, m_new = _state_update(
                jnp.zeros((M_QK_HEAD, M_V_HEAD), F32), jnp.zeros((1, M_QK_HEAD), F32),
                jnp.zeros((1, 1), F32), k, v, li_col, b_col, b_last)
            c_scr[h] = c_new
            n_scr[h] = n_new
            m_scr[h] = jnp.broadcast_to(m_new, (1, LANES))

    tri = _lower_tri(t)
    causal = (lax.broadcasted_iota(jnp.int32, (t, t), 0)
              >= lax.broadcasted_iota(jnp.int32, (t, t), 1))

    def chunk_rows(ci):
        start = ci * t
        return pl.ds(start if isinstance(start, int) else pl.multiple_of(start, t), t)

    def front(ci):
        rows = chunk_rows(ci)
        xn = _rmsnorm(x_ref[rows, :], nw).astype(BF16)
        xn_ref[rows, :] = xn
        la = _log_gates(proj(xn, A_GATE, GATE_PAD), gbias)
        cum = _cumsum_rows(la, tri)
        la_scr[ci] = la
        cum_scr[ci] = cum
        lat_scr[ci] = la.T
        cumt_scr[ci] = cum.T

    def heads(ci):
        rows = chunk_rows(ci)
        xn = xn_ref[rows, :]
        la = la_scr[ci]
        cum = cum_scr[ci]
        la_t = lat_scr[ci]
        cum_t = cumt_scr[ci]
        st = [dict() for _ in range(M_HEADS)]
        acc = [jnp.zeros((t, D_MODEL), F32)]

        def qkv(h):
            d = st[h]
            d["q"] = (proj(xn, A_Q + h * M_QK_HEAD, M_QK_HEAD) * (M_QK_HEAD ** -0.5)).astype(BF16)
            d["k"] = proj(xn, A_K + h * M_QK_HEAD, M_QK_HEAD)
            d["v"] = proj(xn, A_V + h * M_V_HEAD, M_V_HEAD).astype(BF16)

        def scores(h):
            d = st[h]
            d["c"] = c_scr[h]
            d["n"] = n_scr[h]
            d["m"] = m_scr[h][:, 0:1]
            d["li_col"] = la[:, h:h + 1]
            d["b_col"] = cum[:, M_HEADS + h:M_HEADS + h + 1]
            li_row = la_t[h:h + 1, :]
            b_row = cum_t[M_HEADS + h:M_HEADS + h + 1, :]
            d["b_last"] = d["b_col"][t - 1:t, :]
            s = lax.dot_general(d["q"], d["k"].astype(BF16), (((1,), (1,)), ((), ())),
                                preferred_element_type=F32)
            dlog = jnp.where(causal, d["b_col"] - b_row + li_row, -jnp.inf)
            inter = d["b_col"] + d["m"]
            d["m_t"] = jnp.maximum(inter, jnp.max(dlog, axis=-1, keepdims=True))
            d["w"] = jnp.exp(dlog - d["m_t"]) * s
            d["wi"] = jnp.exp(inter - d["m_t"])

        def readout(h):
            d = st[h]
            num = d["wi"] * _dot(d["q"], d["c"].astype(BF16)) + _dot(d["w"].astype(BF16), d["v"])
            qn = jnp.sum(d["q"].astype(F32) * d["n"], axis=-1, keepdims=True)
            den = d["wi"] * qn + jnp.sum(d["w"], axis=-1, keepdims=True)
            d["hh"] = num / jnp.maximum(jnp.abs(den), jnp.exp(-d["m_t"]))

        def update(h):
            d = st[h]
            c_new, n_new, m_new = _state_update(d["c"], d["n"], d["m"], d["k"], d["v"],
                                                d["li_col"], d["b_col"], d["b_last"])
            c_scr[h] = c_new
            n_scr[h] = n_new
            m_scr[h] = jnp.broadcast_to(m_new, (1, LANES))

        def gate(h):
            d = st[h]
            vs = slice(h * M_V_HEAD, (h + 1) * M_V_HEAD)
            og = proj(xn, A_O + h * M_V_HEAD, M_V_HEAD)
            za = proj(xn, A_ZA + h * M_V_HEAD, M_V_HEAD)
            hh = d["hh"]
            hn = (hh * lax.rsqrt(jnp.mean(hh * hh, axis=-1, keepdims=True) + NORM_EPS)
                  * mhw_ref[:, vs])
            d["a"] = ((za * _sigmoid(za)) * (_sigmoid(og) * hn)).astype(BF16)

        def branch(h):
            vs = slice(h * M_V_HEAD, (h + 1) * M_V_HEAD)
            acc[0] = acc[0] + _dot(st[h]["a"], wpa_ref[vs, :])

        stages = (qkv, scores, readout, update, gate, branch)
        for step in range(M_HEADS + len(stages) - 1):
            for si, stage in enumerate(stages):
                h = step - si
                if 0 <= h < M_HEADS:
                    stage(h)
        pa_ref[rows, :] = acc[0]

    front(0)

    def body(ci, carry):
        heads(ci)
        front(ci + 1)
        return carry

    lax.fori_loop(0, n_chunks - 1, body, 0)
    heads(n_chunks - 1)


def _mlstm_side(x2d, meta, nw, wa, wpa, gbias, mhw, bsz, seq):
    nt = seq // MLSTM_ROWS
    n_chunks = MLSTM_ROWS // CHUNK
    rows = lambda b, i: (b * nt + i, 0)
    return pl.pallas_call(
        _mlstm_side_kernel,
        out_shape=(jax.ShapeDtypeStruct((bsz * seq, D_MODEL), F32),
                   jax.ShapeDtypeStruct((bsz * seq, D_MODEL), BF16)),
        grid=(bsz, nt),
        in_specs=[
            pl.BlockSpec((MLSTM_ROWS, D_MODEL), rows),
            _resident((N_META, D_MODEL)),
            _resident((1, D_MODEL)),
            _resident((D_MODEL, A_COLS)),
            _resident((M_V, D_MODEL)),
            _resident((1, GATE_PAD)),
            _resident((1, M_V)),
        ],
        out_specs=(pl.BlockSpec((MLSTM_ROWS, D_MODEL), rows),
                   pl.BlockSpec((MLSTM_ROWS, D_MODEL), rows)),
        scratch_shapes=[
            pltpu.VMEM((M_HEADS, M_QK_HEAD, M_V_HEAD), F32),
            pltpu.VMEM((M_HEADS, 1, M_QK_HEAD), F32),
            pltpu.VMEM((M_HEADS, 1, LANES), F32),
            pltpu.VMEM((n_chunks, CHUNK, GATE_PAD), F32),
            pltpu.VMEM((n_chunks, CHUNK, GATE_PAD), F32),
            pltpu.VMEM((n_chunks, GATE_PAD, CHUNK), F32),
            pltpu.VMEM((n_chunks, GATE_PAD, CHUNK), F32),
        ],
        compiler_params=pltpu.CompilerParams(
            dimension_semantics=("arbitrary", "arbitrary"),
            vmem_limit_bytes=VMEM_LIMIT_BYTES),
        name="mlstm_side",
    )(x2d, meta, nw, wa, wpa, gbias, mhw)


def _conv_side_kernel(x_ref, xn_ref, pa_ref, meta_ref, nw_ref, wb_ref, cw_ref, wpb_ref, wo_ref,
                      fnw_ref, out_ref, u_scr, hist_scr):
    t = CONV_TILE
    nw = nw_ref[...]

    def proj(xn, off, width):
        return _dot(xn, wb_ref[:, off:off + width])

    @pl.when(pl.program_id(1) == 0)
    def _():
        xm = _rmsnorm(meta_ref[...], nw).astype(BF16)
        um = proj(xm, B_SC, S_WIDTH) * proj(xm, B_SH, S_WIDTH)
        hist_scr[...] = um[N_META - CONV_HIST:, :]

    xn = xn_ref[...]
    acc_b = jnp.zeros((t, D_MODEL), F32)
    for cb in range(S_WIDTH // CONV_CB):
        cs = slice(cb * CONV_CB, (cb + 1) * CONV_CB)
        u = proj(xn, B_SC + cb * CONV_CB, CONV_CB) * proj(xn, B_SH + cb * CONV_CB, CONV_CB)
        u_scr[0:CONV_HIST, :] = hist_scr[:, cs]
        u_scr[CONV_HIST:, :] = u
        hist_scr[:, cs] = u[t - CONV_HIST:, :]
        conv = cw_ref[CONV_W - 1:CONV_W, cs] * u
        for j in range(CONV_W - 1):
            shift = CONV_W - 1 - j
            conv = conv + cw_ref[j:j + 1, cs] * u_scr[pl.ds(CONV_HIST - shift, t), :]
        zs = proj(xn, B_ZS + cb * CONV_CB, CONV_CB)
        sb = proj(xn, B_SB + cb * CONV_CB, CONV_CB)
        ys = (zs * _sigmoid(zs)) * (sb * conv)
        acc_b = acc_b + _dot(ys.astype(BF16), wpb_ref[cs, :])

    ga = proj(xn, B_GA, D_MODEL)
    gb = proj(xn, B_GB, D_MODEL)
    merged = _sigmoid(ga) * pa_ref[...] + _sigmoid(gb) * acc_b
    y = x_ref[...] + _dot(merged.astype(BF16), wo_ref[...])
    out_ref[...] = _rmsnorm(y, fnw_ref[...])


def _conv_side(x2d, xn, pa, meta, nw, wb, conv_w, wpb, wo, fnw, bsz, seq):
    nt = seq // CONV_TILE
    rows = lambda b, i: (b * nt + i, 0)
    return pl.pallas_call(
        _conv_side_kernel,
        out_shape=jax.ShapeDtypeStruct((bsz * seq, D_MODEL), F32),
        grid=(bsz, nt),
        in_specs=[
            pl.BlockSpec((CONV_TILE, D_MODEL), rows),
            pl.BlockSpec((CONV_TILE, D_MODEL), rows),
            pl.BlockSpec((CONV_TILE, D_MODEL), rows),
            _resident((N_META, D_MODEL)),
            _resident((1, D_MODEL)),
            _resident((D_MODEL, B_COLS)),
            _resident((CONV_W, S_WIDTH)),
            _resident((S_WIDTH, D_MODEL)),
            _resident((D_MODEL, D_MODEL)),
            _resident((1, D_MODEL)),
        ],
        out_specs=pl.BlockSpec((CONV_TILE, D_MODEL), rows),
        scratch_shapes=[
            pltpu.VMEM((CONV_TILE + CONV_HIST, CONV_CB), F32),
            pltpu.VMEM((CONV_HIST, S_WIDTH), F32),
        ],
        compiler_params=pltpu.CompilerParams(
            dimension_semantics=("arbitrary", "arbitrary"),
            vmem_limit_bytes=VMEM_LIMIT_BYTES),
        name="conv_side",
    )(x2d, xn, pa, meta, nw, wb, conv_w, wpb, wo, fnw)


def kernel(x, meta_tokens, norm_w, w_in, b_igate, b_fgate, mh_norm_w, conv_w,
           w_proj_a, w_proj_b, w_out, final_norm_w):
    bsz, seq, d = x.shape
    assert d == D_MODEL and seq % MLSTM_ROWS == 0 and MLSTM_ROWS % CHUNK == 0 and seq % CONV_TILE == 0
    assert meta_tokens.shape == (N_META, D_MODEL)
    assert norm_w.shape[0] == 1, "single-layer trunk"
    assert w_in.shape[-1] == A_COLS_IN_W + B_COLS
    assert A_COLS % PACK_A_COLS == 0 and A_GATE % PACK_B_COLS == 0 and B_COLS % PACK_B_COLS == 0
    assert PACK_A_COLS % PACK_SUB == 0 and PACK_B_COLS % PACK_SUB == 0
    assert N_GATES == F32_SUBLANES, "pack_wb relies on the gate rows filling one f32 sublane tile"

    wa, wb = _pack_weights(jnp.swapaxes(w_in[0], 0, 1))
    gbias = jnp.pad(jnp.concatenate([b_igate[0], b_fgate[0]]).astype(F32),
                    (0, GATE_PAD - N_GATES)).reshape(1, GATE_PAD)
    nw = norm_w[0].reshape(1, D_MODEL).astype(F32)
    mhw = mh_norm_w[0].reshape(1, M_V).astype(F32)
    fnw = final_norm_w.reshape(1, D_MODEL).astype(F32)
    meta = meta_tokens.astype(F32)

    x2d = x.reshape(bsz * seq, D_MODEL)
    pa, xn = _mlstm_side(x2d, meta, nw, wa, w_proj_a[0].astype(BF16), gbias, mhw, bsz, seq)
    out = _conv_side(x2d, xn, pa, meta, nw, wb, conv_w[0].astype(F32), w_proj_b[0].astype(BF16),
                     w_out[0].astype(BF16), fnw, bsz, seq)
    return out.reshape(bsz, seq, D_MODEL)
```

```python
import jax
import jax.numpy as jnp
from jax import lax
from jax.experimental import pallas as pl
from jax.experimental.pallas import tpu as pltpu

D_MODEL = 1024
N_META = 16
NORM_EPS = 1e-6
M_HEADS = 4
M_V = 2 * D_MODEL
M_V_HEAD = M_V // M_HEADS
M_QK_HEAD = M_V_HEAD // 2
M_QK = M_HEADS * M_QK_HEAD
GATE_SOFTCAP = 15.0
S_WIDTH = 2 * D_MODEL
CONV_W = 3

LANES = 128
GATE_PAD = LANES
N_GATES = 2 * M_HEADS
A_Q = 0
A_K = A_Q + M_QK
A_V = A_K + M_QK
A_O = A_V + M_V
A_ZA = A_O + M_V
A_GATE = A_ZA + M_V
A_COLS_IN_W = A_GATE + N_GATES
A_COLS = A_GATE + GATE_PAD
B_SH = 0
B_SB = B_SH + S_WIDTH
B_SC = B_SB + S_WIDTH
B_ZS = B_SC + S_WIDTH
B_GA = B_ZS + S_WIDTH
B_GB = B_GA + D_MODEL
B_COLS = B_GB + D_MODEL

VMEM_LIMIT_BYTES = 56 * 1024 * 1024

CHUNK = 256
MLSTM_ROWS = 512
FRONT_LEAD = 2
CONV_TILE = 512
CONV_CB = 512
CONV_HIST = 8
PACK_A_COLS = 1664
PACK_B_COLS = 512
PACK_SUB = 128
F32_SUBLANES = 8

BF16 = jnp.bfloat16
F32 = jnp.float32


def _sigmoid(x):
    return 1.0 / (1.0 + jnp.exp(-x))


def _rmsnorm(x, w):
    ms = jnp.mean(x * x, axis=-1, keepdims=True)
    return x * lax.rsqrt(ms + NORM_EPS) * w


def _dot(a, b):
    return jnp.dot(a, b, preferred_element_type=F32)


def _resident(shape):
    return pl.BlockSpec(shape, lambda b, i: (0,) * len(shape), pipeline_mode=pl.Buffered(1))


def _pack_wa_kernel(wt_ref, o_ref):
    for r in range(0, PACK_A_COLS, PACK_SUB):
        o_ref[:, r:r + PACK_SUB] = wt_ref[r:r + PACK_SUB, :].T.astype(BF16)


def _pack_wb_block(wt_ref, nxt_ref, o_ref):
    for r in range(0, PACK_B_COLS, PACK_SUB):
        if r + PACK_SUB < PACK_B_COLS:
            blk = wt_ref[N_GATES + r:N_GATES + r + PACK_SUB, :]
        else:
            blk = jnp.concatenate([wt_ref[N_GATES + r:, :], nxt_ref[...]], axis=0)
        o_ref[:, r:r + PACK_SUB] = blk.T.astype(BF16)


def _pack_wa(wt):
    return pl.pallas_call(
        _pack_wa_kernel,
        out_shape=jax.ShapeDtypeStruct((D_MODEL, A_COLS), BF16),
        grid=(A_COLS // PACK_A_COLS,),
        in_specs=[pl.BlockSpec((PACK_A_COLS, D_MODEL), lambda c: (c, 0))],
        out_specs=pl.BlockSpec((D_MODEL, PACK_A_COLS), lambda c: (0, c)),
        compiler_params=pltpu.CompilerParams(dimension_semantics=("arbitrary",),
                                             vmem_limit_bytes=VMEM_LIMIT_BYTES),
        name="pack_wa",
    )(wt)


def _split3(x):
    hi = x.astype(BF16)
    r1 = x - hi.astype(F32)
    mid = r1.astype(BF16)
    lo = (r1 - mid.astype(F32)).astype(BF16)
    return hi, mid, lo


def _log_gates(g, bias):
    z = GATE_SOFTCAP * jnp.tanh((g + bias) / GATE_SOFTCAP)
    log_f = jnp.minimum(z, 0.0) - jnp.log1p(jnp.exp(-jnp.abs(z)))
    lane = lax.broadcasted_iota(jnp.int32, g.shape, 1)
    return jnp.where(lane < M_HEADS, z, jnp.where(lane < N_GATES, log_f, 0.0))


def _lower_tri(t):
    r = lax.broadcasted_iota(jnp.int32, (t, t), 0)
    c = lax.broadcasted_iota(jnp.int32, (t, t), 1)
    return jnp.where(r >= c, 1.0, 0.0).astype(BF16)


def _cumsum_rows(a, tri):
    hi, mid, lo = _split3(a)
    return _dot(tri, lo) + _dot(tri, mid) + _dot(tri, hi)


def _state_update(c_prev, n_prev, m_prev, k, v, li_col, b_col, b_last):
    wlog = b_last - b_col + li_col
    m_new = jnp.maximum(b_last + m_prev, jnp.max(wlog, axis=0, keepdims=True))
    decay = jnp.exp(b_last + m_prev - m_new)
    kw = jnp.exp(wlog - m_new) * k
    c_new = decay * c_prev + lax.dot_general(
        kw.astype(BF16), v, (((0,), (0,)), ((), ())), preferred_element_type=F32)
    n_new = decay * n_prev + jnp.sum(kw, axis=0, keepdims=True)
    return c_new, n_new, m_new


def _mlstm_side_kernel(x_ref, meta_ref, nw_ref, wa_ref, wpa_ref, gbias_ref, mhw_ref,
                       wt_ref, wt_nxt_ref, pa_ref, xn_ref, wb_ref, c_scr, n_scr, m_scr):
    t = CHUNK
    n_chunks = MLSTM_ROWS // t
    gbias = gbias_ref[...]
    nw = nw_ref[...]

    step = pl.program_id(0) * pl.num_programs(1) + pl.program_id(1)

    @pl.when(step < B_COLS // PACK_B_COLS)
    def _():
        _pack_wb_block(wt_ref, wt_nxt_ref, wb_ref)

    def proj(xn, off, width):
        return _dot(xn, wa_ref[:, off:off + width])

    @pl.when(pl.program_id(1) == 0)
    def _():
        xm = _rmsnorm(meta_ref[...], nw).astype(BF16)
        la = _log_gates(proj(xm, A_GATE, GATE_PAD), gbias)
        cum = _cumsum_rows(la, _lower_tri(N_META))
        for h in range(M_HEADS):
            li_col = la[:, h:h + 1]
            b_col = cum[:, M_HEADS + h:M_HEADS + h + 1]
            b_last = b_col[N_META - 1:N_META, :]
            k = proj(xm, A_K + h * M_QK_HEAD, M_QK_HEAD)
            v = proj(xm, A_V + h * M_V_HEAD, M_V_HEAD).astype(BF16)
            c_new, n_new, m_new = _state_update(
                jnp.zeros((M_QK_HEAD, M_V_HEAD), F32), jnp.zeros((1, M_QK_HEAD), F32),
                jnp.zeros((1, 1), F32), k, v, li_col, b_col, b_last)
            c_scr[h] = c_new
            n_scr[h] = n_new
            m_scr[h] = jnp.broadcast_to(m_new, (1, LANES))

    tri = _lower_tri(t)
    causal = (lax.broadcasted_iota(jnp.int32, (t, t), 0)
              >= lax.broadcasted_iota(jnp.int32, (t, t), 1))

    def front(ci):
        rows = pl.ds(ci * t, t)
        xn = _rmsnorm(x_ref[rows, :], nw).astype(BF16)
        xn_ref[rows, :] = xn
        la = _log_gates(proj(xn, A_GATE, GATE_PAD), gbias)
        cum = _cumsum_rows(la, tri)
        return dict(rows=rows, xn=xn, la=la, cum=cum, la_t=la.T, cum_t=cum.T,
                    acc=jnp.zeros((t, D_MODEL), F32))

    ctx = {}
    st = {}

    def qkv(ci, h):
        xn = ctx[ci]["xn"]
        d = st[ci, h] = {}
        d["q"] = (proj(xn, A_Q + h * M_QK_HEAD, M_QK_HEAD) * (M_QK_HEAD ** -0.5)).astype(BF16)
        d["k"] = proj(xn, A_K + h * M_QK_HEAD, M_QK_HEAD)
        d["v"] = proj(xn, A_V + h * M_V_HEAD, M_V_HEAD).astype(BF16)

    def scores(ci, h):
        c, d = ctx[ci], st[ci, h]
        d["c"] = c_scr[h]
        d["n"] = n_scr[h]
        d["m"] = m_scr[h][:, 0:1]
        d["li_col"] = c["la"][:, h:h + 1]
        d["b_col"] = c["cum"][:, M_HEADS + h:M_HEADS + h + 1]
        li_row = c["la_t"][h:h + 1, :]
        b_row = c["cum_t"][M_HEADS + h:M_HEADS + h + 1, :]
        d["b_last"] = d["b_col"][t - 1:t, :]
        s = lax.dot_general(d["q"], d["k"].astype(BF16), (((1,), (1,)), ((), ())),
                            preferred_element_type=F32)
        dlog = jnp.where(causal, d["b_col"] - b_row + li_row, -jnp.inf)
        inter = d["b_col"] + d["m"]
        d["m_t"] = jnp.maximum(inter, jnp.max(dlog, axis=-1, keepdims=True))
        d["w"] = jnp.exp(dlog - d["m_t"]) * s
        d["wi"] = jnp.exp(inter - d["m_t"])

    def readout(ci, h):
        d = st[ci, h]
        num = d["wi"] * _dot(d["q"], d["c"].astype(BF16)) + _dot(d["w"].astype(BF16), d["v"])
        qn = jnp.sum(d["q"].astype(F32) * d["n"], axis=-1, keepdims=True)
        den = d["wi"] * qn + jnp.sum(d["w"], axis=-1, keepdims=True)
        d["hh"] = num / jnp.maximum(jnp.abs(den), jnp.exp(-d["m_t"]))

    def update(ci, h):
        d = st[ci, h]
        c_new, n_new, m_new = _state_update(d["c"], d["n"], d["m"], d["k"], d["v"],
                                            d["li_col"], d["b_col"], d["b_last"])
        c_scr[h] = c_new
        n_scr[h] = n_new
        m_scr[h] = jnp.broadcast_to(m_new, (1, LANES))

    def gate(ci, h):
        d = st[ci, h]
        xn = ctx[ci]["xn"]
        vs = slice(h * M_V_HEAD, (h + 1) * M_V_HEAD)
        og = proj(xn, A_O + h * M_V_HEAD, M_V_HEAD)
        za = proj(xn, A_ZA + h * M_V_HEAD, M_V_HEAD)
        hh = d["hh"]
        hn = (hh * lax.rsqrt(jnp.mean(hh * hh, axis=-1, keepdims=True) + NORM_EPS)
              * mhw_ref[:, vs])
        d["a"] = ((za * _sigmoid(za)) * (_sigmoid(og) * hn)).astype(BF16)

    def branch(ci, h):
        c = ctx[ci]
        vs = slice(h * M_V_HEAD, (h + 1) * M_V_HEAD)
        c["acc"] = c["acc"] + _dot(st[ci, h]["a"], wpa_ref[vs, :])
        if h == M_HEADS - 1:
            pa_ref[c["rows"], :] = c["acc"]

    stages = (qkv, scores, readout, update, gate, branch)
    items = [(ci, h) for ci in range(n_chunks) for h in range(M_HEADS)]
    ctx[0] = front(0)
    for step in range(len(items) + len(stages) - 1):
        for si, stage in enumerate(stages):
            if 0 <= step - si < len(items):
                stage(*items[step - si])
        nxt = step // M_HEADS + 1
        if step % M_HEADS == FRONT_LEAD and nxt < n_chunks:
            ctx[nxt] = front(nxt)


def _mlstm_side(x2d, meta, nw, wa, wpa, gbias, mhw, wt, bsz, seq):
    nt = seq // MLSTM_ROWS
    n_pack = B_COLS // PACK_B_COLS
    assert bsz * nt >= n_pack
    rows = lambda b, i: (b * nt + i, 0)
    pack_blk = lambda b, i: jnp.minimum(b * nt + i, n_pack - 1)
    first = A_GATE // PACK_B_COLS
    return pl.pallas_call(
        _mlstm_side_kernel,
        out_shape=(jax.ShapeDtypeStruct((bsz * seq, D_MODEL), F32),
                   jax.ShapeDtypeStruct((bsz * seq, D_MODEL), BF16),
                   jax.ShapeDtypeStruct((D_MODEL, B_COLS), BF16)),
        grid=(bsz, nt),
        in_specs=[
            pl.BlockSpec((MLSTM_ROWS, D_MODEL), rows),
            _resident((N_META, D_MODEL)),
            _resident((1, D_MODEL)),
            _resident((D_MODEL, A_COLS)),
            _resident((M_V, D_MODEL)),
            _resident((1, GATE_PAD)),
            _resident((1, M_V)),
            pl.BlockSpec((PACK_B_COLS, D_MODEL), lambda b, i: (first + pack_blk(b, i), 0)),
            pl.BlockSpec((N_GATES, D_MODEL),
                         lambda b, i: ((A_GATE + (pack_blk(b, i) + 1) * PACK_B_COLS) // N_GATES, 0)),
        ],
        out_specs=(pl.BlockSpec((MLSTM_ROWS, D_MODEL), rows),
                   pl.BlockSpec((MLSTM_ROWS, D_MODEL), rows),
                   pl.BlockSpec((D_MODEL, PACK_B_COLS), lambda b, i: (0, pack_blk(b, i)))),
        scratch_shapes=[
            pltpu.VMEM((M_HEADS, M_QK_HEAD, M_V_HEAD), F32),
            pltpu.VMEM((M_HEADS, 1, M_QK_HEAD), F32),
            pltpu.VMEM((M_HEADS, 1, LANES), F32),
        ],
        compiler_params=pltpu.CompilerParams(
            dimension_semantics=("arbitrary", "arbitrary"),
            vmem_limit_bytes=VMEM_LIMIT_BYTES),
        name="mlstm_side",
    )(x2d, meta, nw, wa, wpa, gbias, mhw, wt, wt)


def _conv_side_kernel(x_ref, xn_ref, pa_ref, meta_ref, nw_ref, wb_ref, cw_ref, wpb_ref, wo_ref,
                      fnw_ref, out_ref, u_scr, hist_scr):
    t = CONV_TILE
    nw = nw_ref[...]

    def proj(xn, off, width):
        return _dot(xn, wb_ref[:, off:off + width])

    @pl.when(pl.program_id(1) == 0)
    def _():
        xm = _rmsnorm(meta_ref[...], nw).astype(BF16)
        um = proj(xm, B_SC, S_WIDTH) * proj(xm, B_SH, S_WIDTH)
        hist_scr[...] = um[N_META - CONV_HIST:, :]

    xn = xn_ref[...]
    acc_b = jnp.zeros((t, D_MODEL), F32)
    for cb in range(S_WIDTH // CONV_CB):
        cs = slice(cb * CONV_CB, (cb + 1) * CONV_CB)
        u = proj(xn, B_SC + cb * CONV_CB, CONV_CB) * proj(xn, B_SH + cb * CONV_CB, CONV_CB)
        u_scr[0:CONV_HIST, :] = hist_scr[:, cs]
        u_scr[CONV_HIST:, :] = u
        hist_scr[:, cs] = u[t - CONV_HIST:, :]
        conv = cw_ref[CONV_W - 1:CONV_W, cs] * u
        for j in range(CONV_W - 1):
            shift = CONV_W - 1 - j
            conv = conv + cw_ref[j:j + 1, cs] * u_scr[pl.ds(CONV_HIST - shift, t), :]
        zs = proj(xn, B_ZS + cb * CONV_CB, CONV_CB)
        sb = proj(xn, B_SB + cb * CONV_CB, CONV_CB)
        ys = (zs * _sigmoid(zs)) * (sb * conv)
        acc_b = acc_b + _dot(ys.astype(BF16), wpb_ref[cs, :])

    ga = proj(xn, B_GA, D_MODEL)
    gb = proj(xn, B_GB, D_MODEL)
    merged = _sigmoid(ga) * pa_ref[...] + _sigmoid(gb) * acc_b
    y = x_ref[...] + _dot(merged.astype(BF16), wo_ref[...])
    out_ref[...] = _rmsnorm(y, fnw_ref[...])


def _conv_side(x2d, xn, pa, meta, nw, wb, conv_w, wpb, wo, fnw, bsz, seq):
    nt = seq // CONV_TILE
    rows = lambda b, i: (b * nt + i, 0)
    return pl.pallas_call(
        _conv_side_kernel,
        out_shape=jax.ShapeDtypeStruct((bsz * seq, D_MODEL), F32),
        grid=(bsz, nt),
        in_specs=[
            pl.BlockSpec((CONV_TILE, D_MODEL), rows),
            pl.BlockSpec((CONV_TILE, D_MODEL), rows),
            pl.BlockSpec((CONV_TILE, D_MODEL), rows),
            _resident((N_META, D_MODEL)),
            _resident((1, D_MODEL)),
            _resident((D_MODEL, B_COLS)),
            _resident((CONV_W, S_WIDTH)),
            _resident((S_WIDTH, D_MODEL)),
            _resident((D_MODEL, D_MODEL)),
            _resident((1, D_MODEL)),
        ],
        out_specs=pl.BlockSpec((CONV_TILE, D_MODEL), rows),
        scratch_shapes=[
            pltpu.VMEM((CONV_TILE + CONV_HIST, CONV_CB), F32),
            pltpu.VMEM((CONV_HIST, S_WIDTH), F32),
        ],
        compiler_params=pltpu.CompilerParams(
            dimension_semantics=("arbitrary", "arbitrary"),
            vmem_limit_bytes=VMEM_LIMIT_BYTES),
        name="conv_side",
    )(x2d, xn, pa, meta, nw, wb, conv_w, wpb, wo, fnw)


def kernel(x, meta_tokens, norm_w, w_in, b_igate, b_fgate, mh_norm_w, conv_w,
           w_proj_a, w_proj_b, w_out, final_norm_w):
    bsz, seq, d = x.shape
    assert d == D_MODEL and seq % MLSTM_ROWS == 0 and MLSTM_ROWS % CHUNK == 0 and seq % CONV_TILE == 0
    assert meta_tokens.shape == (N_META, D_MODEL)
    assert norm_w.shape[0] == 1, "single-layer trunk"
    assert w_in.shape[-1] == A_COLS_IN_W + B_COLS
    assert A_COLS % PACK_A_COLS == 0 and A_GATE % PACK_B_COLS == 0 and B_COLS % PACK_B_COLS == 0
    assert PACK_A_COLS % PACK_SUB == 0 and PACK_B_COLS % PACK_SUB == 0
    assert N_GATES == F32_SUBLANES, "pack_wb relies on the gate rows filling one f32 sublane tile"

    wt = jnp.swapaxes(w_in[0], 0, 1)
    wa = _pack_wa(wt)
    gbias = jnp.pad(jnp.concatenate([b_igate[0], b_fgate[0]]).astype(F32),
                    (0, GATE_PAD - N_GATES)).reshape(1, GATE_PAD)
    nw = norm_w[0].reshape(1, D_MODEL).astype(F32)
    mhw = mh_norm_w[0].reshape(1, M_V).astype(F32)
    fnw = final_norm_w.reshape(1, D_MODEL).astype(F32)
    meta = meta_tokens.astype(F32)

    x2d = x.reshape(bsz * seq, D_MODEL)
    pa, xn, wb = _mlstm_side(x2d, meta, nw, wa, w_proj_a[0].astype(BF16), gbias, mhw, wt, bsz, seq)
    out = _conv_side(x2d, xn, pa, meta, nw, wb, conv_w[0].astype(F32), w_proj_b[0].astype(BF16),
                     w_out[0].astype(BF16), fnw, bsz, seq)
    return out.reshape(bsz, seq, D_MODEL)
```
